```python
import math
import jax, jax.numpy as jnp
from jax import lax
import numpy as np

D_MODEL = 2048
BATCH = 4
SEQ = 2048
DEPTH = 1
DEC_BATCH = 4
DEC_SEQ = 4096
PAST_LEN = 128

H_A = 8
DK_A = 128
DV_A = 128
W_A = H_A * DV_A
CONV_DIM = 2 * H_A * DK_A + H_A * DV_A
KCONV = 5
CHUNK = 64
H_B = 8
Q_LORA = 1536
KV_LORA = 512
D_NOPE = 128
D_ROPE = 64
DV_B = 128
W_B = H_B * DV_B
Q_BLOCK = 128
ROPE_THETA = 10000.0
EPS = 1e-6
IN_SPLITS = (CONV_DIM, W_A, H_A, H_A, H_A, H_A, Q_LORA, KV_LORA, D_ROPE, W_B, D_MODEL, D_MODEL)
N_IN = sum(IN_SPLITS)

kernel_name = 'hybrid_gdn_mla_gated_encoder'


def rmsnorm(x, g):
    xf = x.astype(jnp.float32)
    y = xf * lax.rsqrt(jnp.mean(xf * xf, axis=-1, keepdims=True) + EPS)
    return (y * g.astype(jnp.float32)).astype(x.dtype)


def l2norm(x):
    xf = x.astype(jnp.float32)
    return xf * lax.rsqrt(jnp.sum(xf * xf, axis=-1, keepdims=True) + EPS)


def depthwise_conv(x, w):
    c = x.shape[-1]
    return lax.conv_general_dilated(x, w[:, None, :].astype(x.dtype), window_strides=(1,), padding='SAME',
                                    dimension_numbers=('NWC', 'WIO', 'NWC'), feature_group_count=c)


def rotary(t, cos, sin):
    half = D_ROPE // 2
    t1, t2 = t[..., :half], t[..., half:]
    return jnp.concatenate([t1 * cos - t2 * sin, t1 * sin + t2 * cos], axis=-1)


def gated_delta_chunked(q, k, v, g, beta):
    b, s, h, dk = q.shape
    dv = v.shape[-1]
    n = s // CHUNK
    f32 = jnp.float32

    def chunks(t):
        return t.astype(f32).reshape(b, n, CHUNK, h, -1).transpose(0, 3, 1, 2, 4)

    qc = chunks(q) * (dk ** -0.5)
    kc = chunks(k)
    vc = chunks(v)
    bc = chunks(beta[..., None])
    gc = jnp.cumsum(chunks(g[..., None])[..., 0], axis=-1)
    lower = jnp.tril(jnp.ones((CHUNK, CHUNK), dtype=bool))
    strict = jnp.tril(jnp.ones((CHUNK, CHUNK), dtype=bool), k=-1)
    diff = gc[..., :, None] - gc[..., None, :]
    decay = jnp.where(lower, jnp.exp(jnp.where(lower, diff, 0.0)), 0.0)
    k_beta = kc * bc
    kk = jnp.einsum('bhnid,bhnjd->bhnij', k_beta, kc) * decay
    a = jnp.where(strict, kk, 0.0) + jnp.eye(CHUNK, dtype=f32)
    rhs = jnp.concatenate([vc * bc, k_beta * jnp.exp(gc)[..., None]], axis=-1)
    sol = lax.linalg.triangular_solve(a, rhs, left_side=True, lower=True, unit_diagonal=True)
    u, w = sol[..., :dv], sol[..., dv:]
    attn = jnp.where(lower, jnp.einsum('bhnid,bhnjd->bhnij', qc, kc) * decay, 0.0)

    def step(state, xs):
        q_i, k_i, u_i, w_i, g_i, a_i = xs
        v_new = u_i - jnp.einsum('bhcd,bhde->bhce', w_i, state)
        o = (jnp.einsum('bhcd,bhde->bhce', q_i * jnp.exp(g_i)[..., None], state)
             + jnp.einsum('bhij,bhje->bhie', a_i, v_new))
        g_last = g_i[..., -1]
        k_dec = k_i * jnp.exp(g_last[..., None] - g_i)[..., None]
        state = state * jnp.exp(g_last)[..., None, None] + jnp.einsum('bhcd,bhce->bhde', k_dec, v_new)
        return state, o

    xs = tuple(jnp.moveaxis(t, 2, 0) for t in (qc, kc, u, w, gc, attn))
    state0 = jnp.zeros((b, h, dk, dv), f32)
    _, o = lax.scan(step, state0, xs)
    return o.transpose(1, 0, 3, 2, 4).reshape(b, s, h, dv)


def latent_attention_blocks(q, k, v):
    b, s, h, d = q.shape
    scale = d ** -0.5
    qb = q.reshape(b, s // Q_BLOCK, Q_BLOCK, h, d).transpose(1, 0, 2, 3, 4)

    def attend(q_blk):
        sc = jnp.einsum('bqhd,bkhd->bhqk', q_blk, k, preferred_element_type=jnp.float32) * scale
        p = jax.nn.softmax(sc, axis=-1).astype(v.dtype)
        return jnp.einsum('bhqk,bkhe->bqhe', p, v)

    o = lax.map(attend, qb)
    return o.transpose(1, 0, 2, 3, 4).reshape(b, s, h * v.shape[-1])


def encoder_layer(x, norm_in, w_in, conv_w, a_log_f, dt_bias_f, a_log_b, dt_bias_b, o_norm_a,
                  q_a_norm, w_q_b, kv_a_norm, w_kv_b, w_pa, w_pb, w_out):
    b, s, _ = x.shape
    f32 = jnp.float32
    xn = rmsnorm(x, norm_in)
    proj = xn @ w_in
    offsets = [int(o) for o in np.cumsum(IN_SPLITS)[:-1]]
    (qkv, gate_a, a_f, a_b, b_f, b_b, q_lat, kv_lat, k_rope, gate_b, gm_a, gm_b) = jnp.split(proj, offsets, axis=-1)

    qkv = jax.nn.silu(depthwise_conv(qkv, conv_w))
    q_a, k_a, v_a = jnp.split(qkv, [H_A * DK_A, 2 * H_A * DK_A], axis=-1)
    q_a = l2norm(q_a.reshape(b, s, H_A, DK_A))
    k_a = l2norm(k_a.reshape(b, s, H_A, DK_A))
    v_a = v_a.reshape(b, s, H_A, DV_A)
    g_f = -jnp.exp(a_log_f.astype(f32)) * jax.nn.softplus(a_f.astype(f32) + dt_bias_f.astype(f32))
    g_b = -jnp.exp(a_log_b.astype(f32)) * jax.nn.softplus(a_b.astype(f32) + dt_bias_b.astype(f32))
    beta_f = jax.nn.sigmoid(b_f.astype(f32))
    beta_b = jax.nn.sigmoid(b_b.astype(f32))
    o_fwd = gated_delta_chunked(q_a, k_a, v_a, g_f, beta_f)
    flip = lambda t: jnp.flip(t, axis=1)
    o_bwd = flip(gated_delta_chunked(flip(q_a), flip(k_a), flip(v_a), flip(g_b), flip(beta_b)))
    o_a = rmsnorm((o_fwd + o_bwd).astype(x.dtype), o_norm_a).reshape(b, s, W_A)
    o_a = o_a * jax.nn.silu(gate_a)

    cq = rmsnorm(q_lat, q_a_norm)
    qh = (cq @ w_q_b).reshape(b, s, H_B, D_NOPE + D_ROPE)
    ckv = rmsnorm(kv_lat, kv_a_norm)
    kvh = (ckv @ w_kv_b).reshape(b, s, H_B, D_NOPE + DV_B)
    pos = jnp.arange(s, dtype=f32)
    inv_freq = ROPE_THETA ** (-jnp.arange(0, D_ROPE, 2, dtype=f32) / D_ROPE)
    ang = pos[:, None] * inv_freq[None, :]
    cos = jnp.cos(ang)[:, None, :].astype(x.dtype)
    sin = jnp.sin(ang)[:, None, :].astype(x.dtype)
    q_pe = rotary(qh[..., D_NOPE:], cos, sin)
    k_pe = rotary(k_rope.reshape(b, s, 1, D_ROPE), cos, sin)
    q_full = jnp.concatenate([qh[..., :D_NOPE], q_pe], axis=-1)
    k_full = jnp.concatenate([kvh[..., :D_NOPE], jnp.broadcast_to(k_pe, (b, s, H_B, D_ROPE))], axis=-1)
    v_mla = kvh[..., D_NOPE:]
    o_b = latent_attention_blocks(q_full, k_full, v_mla) * jax.nn.silu(gate_b)

    m = jax.nn.sigmoid(gm_a) * (o_a @ w_pa) + jax.nn.sigmoid(gm_b) * (o_b @ w_pb)
    return x + m @ w_out


def setup_inputs(seed: int = 0) -> dict:
    key = jax.random.key(seed)
    ks = jax.random.split(key, 24)
    nrm = lambda k, shape, fan: jax.random.normal(k, shape, jnp.float32) * (fan ** -0.5)
    gain = lambda k, shape: 1.0 + 0.02 * jax.random.normal(k, shape, jnp.float32)

    def dt_bias(k):
        dt = jnp.exp(jax.random.uniform(k, (DEPTH, H_A), jnp.float32) * (math.log(0.1) - math.log(0.001)) + math.log(0.001))
        return dt + jnp.log(-jnp.expm1(-dt))

    def a_log(k):
        return jnp.log(jax.random.uniform(k, (DEPTH, H_A), jnp.float32, 1.0, 16.0))

    return {
        'x_prompt': jax.random.normal(ks[0], (BATCH, SEQ, D_MODEL), jnp.float32),
        'x_sample': jax.random.normal(ks[1], (DEC_BATCH, DEC_SEQ, D_MODEL), jnp.float32),
        'norm_in': gain(ks[2], (DEPTH, D_MODEL)),
        'w_in': nrm(ks[3], (DEPTH, D_MODEL, N_IN), D_MODEL),
        'conv_w': nrm(ks[4], (DEPTH, KCONV, CONV_DIM), KCONV),
        'a_log_f': a_log(ks[5]),
        'dt_bias_f': dt_bias(ks[6]),
        'a_log_b': a_log(ks[7]),
        'dt_bias_b': dt_bias(ks[8]),
        'o_norm_a': gain(ks[9], (DEPTH, DV_A)),
        'q_a_norm': gain(ks[10], (DEPTH, Q_LORA)),
        'w_q_b': nrm(ks[11], (DEPTH, Q_LORA, H_B * (D_NOPE + D_ROPE)), Q_LORA),
        'kv_a_norm': gain(ks[12], (DEPTH, KV_LORA)),
        'w_kv_b': nrm(ks[13], (DEPTH, KV_LORA, H_B * (D_NOPE + DV_B)), KV_LORA),
        'w_pa': nrm(ks[14], (DEPTH, W_A, D_MODEL), W_A),
        'w_pb': nrm(ks[15], (DEPTH, W_B, D_MODEL), W_B),
        'w_out': nrm(ks[16], (DEPTH, D_MODEL, D_MODEL), D_MODEL),
        'norm_f': gain(ks[17], (D_MODEL,)),
    }


def reference(x_prompt, x_sample, norm_in, w_in, conv_w, a_log_f, dt_bias_f, a_log_b, dt_bias_b, o_norm_a,
              q_a_norm, w_q_b, kv_a_norm, w_kv_b, w_pa, w_pb, w_out, norm_f):
    def trunk(x):
        for l in range(DEPTH):
            x = encoder_layer(x, norm_in[l], w_in[l], conv_w[l], a_log_f[l], dt_bias_f[l], a_log_b[l], dt_bias_b[l],
                              o_norm_a[l], q_a_norm[l], w_q_b[l], kv_a_norm[l], w_kv_b[l], w_pa[l], w_pb[l], w_out[l])
        return rmsnorm(x, norm_f)

    y_prompt = trunk(x_prompt)
    y_sample = trunk(x_sample)
    return (y_prompt, y_sample)
```

```python
import functools

import jax
import jax.numpy as jnp
import numpy as np
from jax import lax
from jax.experimental import pallas as pl
from jax.experimental.pallas import tpu as pltpu

F32 = jnp.float32
BF16 = jnp.bfloat16

D_MODEL = 2048
H_A = 8
DK_A = 128
DV_A = 128
W_A = H_A * DV_A
CONV_DIM = 2 * H_A * DK_A + H_A * DV_A
KCONV = 5
H_B = 8
Q_LORA = 1536
KV_LORA = 512
D_NOPE = 128
D_ROPE = 64
DV_B = 128
W_B = H_B * DV_B
ROPE_THETA = 10000.0
EPS = 1e-6

LANE = 128
CHUNK = 128

OFF_QKV = 0
OFF_GATE_A = 3072
OFF_GM_A = 4096
OFF_GM_B = 6144
OFF_GATE_B = 8192
OFF_QLAT = 9216
OFF_KVLAT = 10752
OFF_SMALL = 11264
N_PROJ = 11520
SMALL_W = 256

VMEM_LIMIT = 56 * 1024 * 1024


def _cparams(sem):
    return pltpu.CompilerParams(dimension_semantics=sem, vmem_limit_bytes=VMEM_LIMIT)


def _sigmoid(x):
    return 1.0 / (1.0 + jnp.exp(-x))


def _silu(x):
    return x * _sigmoid(x)


def _inproj_kernel(x_ref, g_ref, w_ref, out_ref, small_ref, xn_ref, *, nj, small_off):
    j = pl.program_id(1)

    @pl.when(j == 0)
    def _():
        xf = x_ref[...]
        ms = jnp.mean(xf * xf, axis=-1, keepdims=True)
        xn_ref[...] = (xf * lax.rsqrt(ms + EPS) * g_ref[...]).astype(BF16)

    acc = jnp.dot(xn_ref[...], w_ref[...], preferred_element_type=F32)
    out_ref[...] = acc.astype(out_ref.dtype)

    @pl.when(j == nj - 1)
    def _():
        small_ref[...] = acc[:, small_off:small_off + SMALL_W]


def _in_proj(x2d, norm_g, w_bf16, tm=512, tn=1152):
    t = x2d.shape[0]
    tm = min(tm, t)
    nj = N_PROJ // tn
    small_off = OFF_SMALL - (nj - 1) * tn
    return pl.pallas_call(
        functools.partial(_inproj_kernel, nj=nj, small_off=small_off),
        grid=(t // tm, nj),
        in_specs=[
            pl.BlockSpec((tm, D_MODEL), lambda i, j: (i, 0)),
            pl.BlockSpec((1, D_MODEL), lambda i, j: (0, 0)),
            pl.BlockSpec((D_MODEL, tn), lambda i, j: (0, j)),
        ],
        out_specs=[
            pl.BlockSpec((tm, tn), lambda i, j: (i, j)),
            pl.BlockSpec((tm, SMALL_W), lambda i, j: (i, 0)),
        ],
        out_shape=[
            jax.ShapeDtypeStruct((t, N_PROJ), BF16),
            jax.ShapeDtypeStruct((t, SMALL_W), F32),
        ],
        scratch_shapes=[pltpu.VMEM((tm, D_MODEL), BF16)],
        compiler_params=_cparams(("parallel", "arbitrary")),
        name="in_proj",
    )(x2d, norm_g, w_bf16)


HALO = 16


def _conv_kernel(prev_ref, main_ref, next_ref, w_ref, out_ref, xp_ref, *, tm, nr):
    r = pl.program_id(1)
    c = pl.program_id(2)
    width = main_ref.shape[1]
    p = prev_ref[...].astype(F32)[HALO - 8:HALO]
    n = next_ref[...].astype(F32)[0:8]
    xp_ref[0:8, :] = jnp.where(r > 0, p, 0.0)
    xp_ref[8:8 + tm, :] = main_ref[...].astype(F32)
    xp_ref[8 + tm:16 + tm, :] = jnp.where(r < nr - 1, n, 0.0)
    w = w_ref[...]
    acc = jnp.zeros((tm, width), F32)
    for j in range(KCONV):
        acc = acc + w[j:j + 1, :] * xp_ref[6 + j:6 + j + tm, :]
    y = _silu(acc)
    norm_on = c < 2
    qscale = jnp.where(c == 0, DK_A ** -0.5, 1.0).astype(F32)
    for h in range(width // LANE):
        seg = y[:, h * LANE:(h + 1) * LANE]
        ss = jnp.sum(seg * seg, axis=-1, keepdims=True)
        scale = jnp.where(norm_on, lax.rsqrt(ss + EPS), 1.0) * qscale
        out_ref[:, h * LANE:(h + 1) * LANE] = (seg * scale).astype(out_ref.dtype)


def _conv_act(proj, conv_w, b, s, tm=512):
    t = b * s
    tm = min(tm, s)
    nr = s // tm
    width = W_A
    hb = tm // HALO

    def prev_map(bi, r, c):
        return (jnp.maximum(bi * (s // HALO) + r * hb - 1, 0), c)

    def next_map(bi, r, c):
        return (jnp.minimum(bi * (s // HALO) + (r + 1) * hb, t // HALO - 1), c)

    return pl.pallas_call(
        functools.partial(_conv_kernel, tm=tm, nr=nr),
        grid=(b, nr, CONV_DIM // width),
        in_specs=[
            pl.BlockSpec((HALO, width), prev_map),
            pl.BlockSpec((tm, width), lambda bi, r, c: (bi * nr + r, c)),
            pl.BlockSpec((HALO, width), next_map),
            pl.BlockSpec((KCONV, width), lambda bi, r, c: (0, c)),
        ],
        out_specs=pl.BlockSpec((tm, width), lambda bi, r, c: (bi * nr + r, c)),
        out_shape=jax.ShapeDtypeStruct((t, CONV_DIM), BF16),
        scratch_shapes=[pltpu.VMEM((tm + 16, width), F32)],
        compiler_params=_cparams(("parallel", "parallel", "parallel")),
        name="conv_act",
    )(proj, proj, proj, conv_w)


N_GATE_BLK = 4


def _gate_kernel(g_ref, alog_ref, dt_ref, cols_ref, rows_ref, *, tm):
    raw = g_ref[...]
    lane = lax.broadcasted_iota(jnp.int32, (CHUNK, LANE), 1)
    z = raw + dt_ref[...]
    softplus = jnp.maximum(z, 0.0) + jnp.log1p(jnp.exp(-jnp.abs(z)))
    g = -jnp.exp(alog_ref[...]) * softplus
    beta = _sigmoid(raw)
    ri = lax.broadcasted_iota(jnp.int32, (CHUNK, CHUNK), 0)
    ci = lax.broadcasted_iota(jnp.int32, (CHUNK, CHUNK), 1)
    lower = (ri >= ci).astype(F32)
    upper = (ri <= ci).astype(F32)
    ones = jnp.ones((CHUNK, CHUNK), F32)
    dot = functools.partial(jnp.dot, precision=lax.Precision.HIGHEST, preferred_element_type=F32)
    for cc in range(tm // CHUNK):
        sl = slice(cc * CHUNK, (cc + 1) * CHUNK)
        gs = g[sl]
        gc = jnp.where(lane < H_A, dot(lower, gs), dot(upper, gs))
        tot = dot(ones, gs)
        blocks = (
            jnp.where(lane < 2 * H_A, gc, beta[sl]),
            jnp.exp(gc),
            jnp.exp(tot - gc),
            jnp.exp(tot),
        )
        for k, blk in enumerate(blocks):
            cols_ref[sl, k * LANE:(k + 1) * LANE] = blk
            rows_ref[k * LANE:(k + 1) * LANE, sl] = blk.T


def _gate_prep(small, alog_row, dt_row, tm=512):
    t = small.shape[0]
    tm = min(tm, t)
    return pl.pallas_call(
        functools.partial(_gate_kernel, tm=tm),
        grid=(t // tm,),
        in_specs=[
            pl.BlockSpec((tm, LANE), lambda i: (i, 1)),
            pl.BlockSpec((1, LANE), lambda i: (0, 0)),
            pl.BlockSpec((1, LANE), lambda i: (0, 0)),
        ],
        out_specs=[
            pl.BlockSpec((tm, N_GATE_BLK * LANE), lambda i: (i, 0)),
            pl.BlockSpec((N_GATE_BLK * LANE, tm), lambda i: (0, i)),
        ],
        out_shape=[
            jax.ShapeDtypeStruct((t, N_GATE_BLK * LANE), F32),
            jax.ShapeDtypeStruct((N_GATE_BLK * LANE, t), F32),
        ],
        compiler_params=_cparams(("parallel",)),
        name="gate_prep",
    )(small, alog_row, dt_row)


def _mm(a, b):
    return jnp.dot(a.astype(BF16), b.astype(BF16), preferred_element_type=F32)


def _delta_unit(q, k, v, cols, rows, s_ref, h, backward, tri, strict_tri, eye, level_masks):
    base = H_A if backward else 0
    gcc = cols[:, base + h:base + h + 1]
    beta = cols[:, 2 * H_A + base + h:2 * H_A + base + h + 1]
    egc = cols[:, LANE + base + h:LANE + base + h + 1]
    gcr = rows[base + h:base + h + 1, :]
    kdr = rows[2 * LANE + base + h:2 * LANE + base + h + 1, :]
    elr = rows[3 * LANE + base + h:3 * LANE + base + h + 1, :]

    kf = k.astype(F32)
    qf = q.astype(F32)
    kbeta = kf * beta
    kt = kf.T
    kq = _mm(jnp.concatenate([kbeta, qf], axis=0), kt)
    decay = jnp.where(tri, jnp.exp(jnp.where(tri, gcc - gcr, 0.0)), 0.0)
    attn = kq[CHUNK:] * decay
    lmat = jnp.where(strict_tri, kq[:CHUNK] * decay, 0.0)
    p = eye - jnp.where(level_masks[0], lmat, 0.0)
    for lm in level_masks[1:]:
        e = jnp.where(lm, lmat, 0.0)
        p = p - _mm(p, _mm(e, p))
    rhs = jnp.concatenate([v.astype(F32) * beta, kbeta * egc], axis=1)
    sol = _mm(p, rhs)
    u = sol[:, :DV_A]
    w = sol[:, DV_A:]
    state = s_ref[...]
    wq = _mm(jnp.concatenate([w, qf * egc], axis=0), state)
    v_new = u - wq[:CHUNK]
    o = wq[CHUNK:] + _mm(attn, v_new)
    s_ref[...] = state * elr + _mm(kt * kdr, v_new)
    return o


def _delta_kernel(qf_ref, kf_ref, vf_ref, qb_ref, kb_ref, vb_ref, colsf_ref, colsb_ref, rowsf_ref, rowsb_ref,
                  of_ref, ob_ref, s_ref, *, hb):
    t = pl.program_id(2)

    @pl.when(t == 0)
    def _():
        s_ref[...] = jnp.zeros_like(s_ref)

    ri = lax.broadcasted_iota(jnp.int32, (CHUNK, CHUNK), 0)
    ci = lax.broadcasted_iota(jnp.int32, (CHUNK, CHUNK), 1)
    eye = (ri == ci).astype(F32)
    level_masks = [((ri >> (l + 1)) == (ci >> (l + 1))) & ((ri >> l) != (ci >> l))
                   for l in range(int(np.log2(CHUNK)))]
    for backward in (False, True):
        q_ref, k_ref, v_ref, cols_ref, rows_ref, o_ref = (
            (qb_ref, kb_ref, vb_ref, colsb_ref, rowsb_ref, ob_ref) if backward
            else (qf_ref, kf_ref, vf_ref, colsf_ref, rowsf_ref, of_ref))
        tri = (ri <= ci) if backward else (ri >= ci)
        strict = (ri < ci) if backward else (ri > ci)
        cols = cols_ref[...]
        rows = rows_ref[...]
        for hh in range(hb):
            sl = slice(hh * LANE, (hh + 1) * LANE)
            o = _delta_unit(q_ref[:, sl], k_ref[:, sl], v_ref[:, sl], cols, rows,
                            s_ref.at[int(backward), hh], hh, backward, tri, strict, eye, level_masks)
            o_ref[:, sl] = o.astype(o_ref.dtype)


def _delta(qkv_act, cols, rows, b, s):
    t = b * s
    n = s // CHUNK
    hb = H_A
    nhb = H_A // hb
    wblk = hb * LANE
    qoff, koff, voff = 0, W_A // wblk, 2 * W_A // wblk

    def fwd(off):
        return lambda bi, hi, ti: (bi * n + ti, off + hi)

    def bwd(off):
        return lambda bi, hi, ti: (bi * n + n - 1 - ti, off + hi)

    qkv_specs = [pl.BlockSpec((CHUNK, wblk), m(off)) for m in (fwd, bwd) for off in (qoff, koff, voff)]
    ncol = N_GATE_BLK * LANE
    return pl.pallas_call(
        functools.partial(_delta_kernel, hb=hb),
        grid=(b, nhb, n),
        in_specs=qkv_specs + [
            pl.BlockSpec((CHUNK, ncol), lambda bi, hi, ti: (bi * n + ti, 0)),
            pl.BlockSpec((CHUNK, ncol), lambda bi, hi, ti: (bi * n + n - 1 - ti, 0)),
            pl.BlockSpec((ncol, CHUNK), lambda bi, hi, ti: (0, bi * n + ti)),
            pl.BlockSpec((ncol, CHUNK), lambda bi, hi, ti: (0, bi * n + n - 1 - ti)),
        ],
        out_specs=[pl.BlockSpec((CHUNK, wblk), fwd(0)), pl.BlockSpec((CHUNK, wblk), bwd(0))],
        out_shape=[jax.ShapeDtypeStruct((t, W_A), BF16), jax.ShapeDtypeStruct((t, W_A), BF16)],
        scratch_shapes=[pltpu.VMEM((2, hb, DK_A, DV_A), F32)],
        compiler_params=_cparams(("parallel", "parallel", "arbitrary")),
        name="delta",
    )(qkv_act, qkv_act, qkv_act, qkv_act, qkv_act, qkv_act, cols, cols, rows, rows)


HEAD_PAD = 2 * LANE


def _mla_prep_kernel(qlat_ref, kvlat_ref, rope_ref, cos_ref, sin_ref, gq_ref, gkv_ref,
                     wqn_ref, wqr_ref, wkn_ref, wv_ref, q_out, k_out, v_out):
    scale = (D_NOPE + D_ROPE) ** -0.5
    cosm = cos_ref[...]
    sinm = sin_ref[...]

    def rms(x, g):
        ms = jnp.mean(x * x, axis=-1, keepdims=True)
        return (x * lax.rsqrt(ms + EPS) * g).astype(BF16)

    def rot(blk):
        return blk * cosm + pltpu.roll(blk, LANE // 2, 1) * sinm

    cq = rms(qlat_ref[...].astype(F32), gq_ref[...])
    qn = jnp.dot(cq, wqn_ref[...], preferred_element_type=F32)
    qr = jnp.dot(cq, wqr_ref[...], preferred_element_type=F32)
    ckv = rms(kvlat_ref[...].astype(F32), gkv_ref[...])
    kn = jnp.dot(ckv, wkn_ref[...], preferred_element_type=F32)
    v_out[...] = jnp.dot(ckv, wv_ref[...], preferred_element_type=F32).astype(v_out.dtype)
    kpe = rot(rope_ref[...]).astype(k_out.dtype)
    for h in range(H_B):
        sl = slice(h * LANE, (h + 1) * LANE)
        q_out[:, h * HEAD_PAD:h * HEAD_PAD + LANE] = (qn[:, sl] * scale).astype(q_out.dtype)
        q_out[:, h * HEAD_PAD + LANE:(h + 1) * HEAD_PAD] = (rot(qr[:, sl]) * scale).astype(q_out.dtype)
        k_out[:, h * HEAD_PAD:h * HEAD_PAD + LANE] = kn[:, sl].astype(k_out.dtype)
        k_out[:, h * HEAD_PAD + LANE:(h + 1) * HEAD_PAD] = kpe


def _mla_prep(proj, small, cosm, sinm, gq, gkv, wqn, wqr, wkn, wv, b, s, tm=512):
    t = b * s
    tm = min(tm, s)
    nr = s // tm
    row = lambda bi, r: bi * nr + r
    const = lambda bi, r: (0, 0)
    return pl.pallas_call(
        _mla_prep_kernel,
        grid=(b, nr),
        in_specs=[
            pl.BlockSpec((tm, Q_LORA), lambda bi, r: (row(bi, r), OFF_QLAT // Q_LORA)),
            pl.BlockSpec((tm, KV_LORA), lambda bi, r: (row(bi, r), OFF_KVLAT // KV_LORA)),
            pl.BlockSpec((tm, LANE), lambda bi, r: (row(bi, r), 0)),
            pl.BlockSpec((tm, LANE), lambda bi, r: (r, 0)),
            pl.BlockSpec((tm, LANE), lambda bi, r: (r, 0)),
            pl.BlockSpec((1, Q_LORA), const),
            pl.BlockSpec((1, KV_LORA), const),
            pl.BlockSpec((Q_LORA, W_B), const),
            pl.BlockSpec((Q_LORA, W_B), const),
            pl.BlockSpec((KV_LORA, W_B), const),
            pl.BlockSpec((KV_LORA, W_B), const),
        ],
        out_specs=[
            pl.BlockSpec((tm, H_B * HEAD_PAD), lambda bi, r: (row(bi, r), 0)),
            pl.BlockSpec((tm, H_B * HEAD_PAD), lambda bi, r: (row(bi, r), 0)),
            pl.BlockSpec((tm, W_B), lambda bi, r: (row(bi, r), 0)),
        ],
        out_shape=[
            jax.ShapeDtypeStruct((t, H_B * HEAD_PAD), BF16),
            jax.ShapeDtypeStruct((t, H_B * HEAD_PAD), BF16),
            jax.ShapeDtypeStruct((t, W_B), BF16),
        ],
        compiler_params=_cparams(("parallel", "parallel")),
        name="mla_prep",
    )(proj, proj, small, cosm, sinm, gq, gkv, wqn, wqr, wkn, wv)


def _attn_kernel(q_ref, k_ref, v_ref, o_ref, m_ref, l_ref, acc_ref, *, nk):
    j = pl.program_id(3)

    @pl.when(j == 0)
    def _():
        m_ref[...] = jnp.full_like(m_ref, -jnp.inf)
        l_ref[...] = jnp.zeros_like(l_ref)
        acc_ref[...] = jnp.zeros_like(acc_ref)

    sc = lax.dot_general(q_ref[...], k_ref[...], (((1,), (1,)), ((), ())), preferred_element_type=F32)
    m_prev = m_ref[...]
    m_new = jnp.maximum(m_prev, jnp.max(sc, axis=-1, keepdims=True))
    alpha = jnp.exp(m_prev - m_new)
    p = jnp.exp(sc - m_new)
    l_ref[...] = alpha * l_ref[...] + jnp.sum(p, axis=-1, keepdims=True)
    acc_ref[...] = alpha * acc_ref[...] + jnp.dot(p.astype(BF16), v_ref[...], preferred_element_type=F32)
    m_ref[...] = m_new

    @pl.when(j == nk - 1)
    def _():
        o_ref[...] = (acc_ref[...] / l_ref[...]).astype(o_ref.dtype)


def _attention(qp, kp, vp, b, s, tq=1024, tk=1024):
    t = b * s
    tq = min(tq, s)
    tk = min(tk, s)
    nq, nk = s // tq, s // tk
    return pl.pallas_call(
        functools.partial(_attn_kernel, nk=nk),
        grid=(b, H_B, nq, nk),
        in_specs=[
            pl.BlockSpec((tq, HEAD_PAD), lambda bi, h, i, j: (bi * nq + i, h)),
            pl.BlockSpec((tk, HEAD_PAD), lambda bi, h, i, j: (bi * nk + j, h)),
            pl.BlockSpec((tk, DV_B), lambda bi, h, i, j: (bi * nk + j, h)),
        ],
        out_specs=pl.BlockSpec((tq, DV_B), lambda bi, h, i, j: (bi * nq + i, h)),
        out_shape=jax.ShapeDtypeStruct((t, W_B), BF16),
        scratch_shapes=[pltpu.VMEM((tq, 1), F32), pltpu.VMEM((tq, 1), F32), pltpu.VMEM((tq, DV_B), F32)],
        compiler_params=_cparams(("parallel", "parallel", "parallel", "arbitrary")),
        name="attention",
    )(qp, kp, vp)


def _merge_kernel(of_ref, ob_ref, ga_ref, attn_ref, gb_ref, gma_ref, gmb_ref, x_ref,
                  onorm_ref, nf_ref, wpa_ref, wpb_ref, wout_ref, y_ref):
    oa = of_ref[...].astype(F32) + ob_ref[...].astype(F32)
    gate_a = _silu(ga_ref[...].astype(F32))
    onorm = onorm_ref[...]
    parts = []
    for h in range(H_A):
        sl = slice(h * DV_A, (h + 1) * DV_A)
        seg = oa[:, sl]
        ms = jnp.mean(seg * seg, axis=-1, keepdims=True)
        parts.append((seg * lax.rsqrt(ms + EPS) * onorm * gate_a[:, sl]).astype(BF16))
    o_a = jnp.concatenate(parts, axis=-1)
    o_b = (attn_ref[...].astype(F32) * _silu(gb_ref[...].astype(F32))).astype(BF16)
    pa = jnp.dot(o_a, wpa_ref[...], preferred_element_type=F32)
    pb = jnp.dot(o_b, wpb_ref[...], preferred_element_type=F32)
    m = _sigmoid(gma_ref[...].astype(F32)) * pa + _sigmoid(gmb_ref[...].astype(F32)) * pb
    y = x_ref[...] + jnp.dot(m.astype(BF16), wout_ref[...], preferred_element_type=F32)
    ms = jnp.mean(y * y, axis=-1, keepdims=True)
    y_ref[...] = y * lax.rsqrt(ms + EPS) * nf_ref[...]


def _merge_out(o_f, o_b, proj, attn, x2d, onorm, nf, wpa, wpb, wout, tm=256):
    t = x2d.shape[0]
    tm = min(tm, t)
    const = lambda i: (0, 0)
    single = pl.Buffered(1)
    return pl.pallas_call(
        _merge_kernel,
        grid=(t // tm,),
        in_specs=[
            pl.BlockSpec((tm, W_A), lambda i: (i, 0)),
            pl.BlockSpec((tm, W_A), lambda i: (i, 0)),
            pl.BlockSpec((tm, W_A), lambda i: (i, OFF_GATE_A // W_A)),
            pl.BlockSpec((tm, W_B), lambda i: (i, 0)),
            pl.BlockSpec((tm, W_B), lambda i: (i, OFF_GATE_B // W_B)),
            pl.BlockSpec((tm, D_MODEL), lambda i: (i, OFF_GM_A // D_MODEL)),
            pl.BlockSpec((tm, D_MODEL), lambda i: (i, OFF_GM_B // D_MODEL)),
            pl.BlockSpec((tm, D_MODEL), lambda i: (i, 0)),
            pl.BlockSpec((1, DV_A), const),
            pl.BlockSpec((1, D_MODEL), const),
            pl.BlockSpec((W_A, D_MODEL), const, pipeline_mode=single),
            pl.BlockSpec((W_B, D_MODEL), const, pipeline_mode=single),
            pl.BlockSpec((D_MODEL, D_MODEL), const, pipeline_mode=single),
        ],
        out_specs=pl.BlockSpec((tm, D_MODEL), lambda i: (i, 0)),
        out_shape=jax.ShapeDtypeStruct((t, D_MODEL), F32),
        compiler_params=_cparams(("parallel",)),
        name="merge_out",
    )(o_f, o_b, proj, attn, proj, proj, proj, x2d, onorm, nf, wpa, wpb, wout)


def _prep_weights(norm_in, w_in, conv_w, a_log_f, dt_bias_f, a_log_b, dt_bias_b, o_norm_a,
                  q_a_norm, w_q_b, kv_a_norm, w_kv_b, w_pa, w_pb, w_out, norm_f):
    w = w_in
    o = np.cumsum((0, CONV_DIM, W_A, H_A, H_A, H_A, H_A, Q_LORA, KV_LORA, D_ROPE, W_B, D_MODEL, D_MODEL))
    seg = lambda i: w[:, int(o[i]):int(o[i + 1])]
    qkv, gate_a, a_f, a_b, b_f, b_b, q_lat, kv_lat, k_rope, gate_b, gm_a, gm_b = (seg(i) for i in range(12))
    half = D_ROPE // 2
    k_rope_sw = jnp.concatenate([k_rope[:, half:], k_rope[:, :half]], axis=1)
    pad = jnp.zeros((D_MODEL, LANE - 4 * H_A), w.dtype)
    w_proj = jnp.concatenate([qkv, gate_a, gm_a, gm_b, gate_b, q_lat, kv_lat, k_rope, k_rope_sw,
                              a_f, a_b, b_f, b_b, pad], axis=1).astype(BF16)
    lane_pad = lambda f, bk: jnp.concatenate([f, bk, jnp.zeros((LANE - 2 * H_A,), F32)])[None, :]
    wq = w_q_b.reshape(Q_LORA, H_B, D_NOPE + D_ROPE)
    wq_rope = wq[:, :, D_NOPE:]
    wq_rope_sw = jnp.concatenate([wq_rope[..., half:], wq_rope[..., :half]], axis=-1)
    wkv = w_kv_b.reshape(KV_LORA, H_B, D_NOPE + DV_B)
    return dict(
        norm_in=norm_in[None, :], w_proj=w_proj, conv_w=conv_w,
        alog=lane_pad(a_log_f, a_log_b), dt=lane_pad(dt_bias_f, dt_bias_b),
        onorm=o_norm_a[None, :], nf=norm_f[None, :],
        gq=q_a_norm[None, :], gkv=kv_a_norm[None, :],
        wqn=wq[:, :, :D_NOPE].reshape(Q_LORA, W_B).astype(BF16),
        wqr=jnp.concatenate([wq_rope, wq_rope_sw], axis=-1).reshape(Q_LORA, W_B).astype(BF16),
        wkn=wkv[:, :, :D_NOPE].reshape(KV_LORA, W_B).astype(BF16),
        wv=wkv[:, :, D_NOPE:].reshape(KV_LORA, W_B).astype(BF16),
        wpa=w_pa.astype(BF16), wpb=w_pb.astype(BF16), wout=w_out.astype(BF16),
    )


def _rope_tables(s):
    pos = jnp.arange(s, dtype=F32)
    inv_freq = ROPE_THETA ** (-jnp.arange(0, D_ROPE, 2, dtype=F32) / D_ROPE)
    ang = pos[:, None] * inv_freq[None, :]
    cos, sin = jnp.cos(ang), jnp.sin(ang)
    z = jnp.zeros((s, LANE // 2), F32)
    return jnp.concatenate([cos, cos, z], axis=1), jnp.concatenate([-sin, sin, z], axis=1)


def _trunk(x, p):
    b, s, _ = x.shape
    x2d = x.reshape(b * s, D_MODEL)
    proj, small = _in_proj(x2d, p["norm_in"], p["w_proj"])
    qkv_act = _conv_act(proj, p["conv_w"], b, s)
    cols, rows = _gate_prep(small, p["alog"], p["dt"])
    o_f, o_b = _delta(qkv_act, cols, rows, b, s)
    cosm, sinm = _rope_tables(s)
    qp, kp, vp = _mla_prep(proj, small, cosm, sinm, p["gq"], p["gkv"], p["wqn"], p["wqr"], p["wkn"], p["wv"], b, s)
    attn = _attention(qp, kp, vp, b, s)
    y = _merge_out(o_f, o_b, proj, attn, x2d, p["onorm"], p["nf"], p["wpa"], p["wpb"], p["wout"])
    return y.reshape(b, s, D_MODEL)


def kernel(x_prompt, x_sample, norm_in, w_in, conv_w, a_log_f, dt_bias_f, a_log_b, dt_bias_b, o_norm_a,
           q_a_norm, w_q_b, kv_a_norm, w_kv_b, w_pa, w_pb, w_out, norm_f):
    assert norm_in.shape[0] == 1, "single-layer trunk"
    p = _prep_weights(norm_in[0], w_in[0], conv_w[0], a_log_f[0], dt_bias_f[0], a_log_b[0], dt_bias_b[0],
                      o_norm_a[0], q_a_norm[0], w_q_b[0], kv_a_norm[0], w_kv_b[0], w_pa[0], w_pb[0], w_out[0],
                      norm_f)
    return (_trunk(x_prompt, p), _trunk(x_sample, p))
```

```python
import functools

import jax
import jax.numpy as jnp
import numpy as np
from jax import lax
from jax.experimental import pallas as pl
from jax.experimental.pallas import tpu as pltpu

F32 = jnp.float32
BF16 = jnp.bfloat16

D_MODEL = 2048
H_A = 8
DK_A = 128
DV_A = 128
W_A = H_A * DV_A
CONV_DIM = 2 * H_A * DK_A + H_A * DV_A
KCONV = 5
H_B = 8
Q_LORA = 1536
KV_LORA = 512
D_NOPE = 128
D_ROPE = 64
DV_B = 128
W_B = H_B * DV_B
ROPE_THETA = 10000.0
EPS = 1e-6

LANE = 128
CHUNK = 128

OFF_QKV = 0
OFF_GATE_A = 3072
OFF_GM_A = 4096
OFF_GM_B = 6144
OFF_GATE_B = 8192
OFF_QLAT = 9216
OFF_KVLAT = 10752
OFF_SMALL = 11264
N_PROJ = 11520
SMALL_W = 256

VMEM_LIMIT = 56 * 1024 * 1024


def _cparams(sem):
    return pltpu.CompilerParams(dimension_semantics=sem, vmem_limit_bytes=VMEM_LIMIT)


def _sigmoid(x):
    return 1.0 / (1.0 + jnp.exp(-x))


def _silu(x):
    return x * _sigmoid(x)


def _inproj_kernel(x_ref, g_ref, w_ref, out_ref, small_ref, xn_ref, *, nj, small_off):
    j = pl.program_id(1)

    @pl.when(j == 0)
    def _():
        xf = x_ref[...]
        ms = jnp.mean(xf * xf, axis=-1, keepdims=True)
        xn_ref[...] = (xf * lax.rsqrt(ms + EPS) * g_ref[...]).astype(BF16)

    acc = jnp.dot(xn_ref[...], w_ref[...], preferred_element_type=F32)
    out_ref[...] = acc.astype(out_ref.dtype)

    @pl.when(j == nj - 1)
    def _():
        small_ref[...] = acc[:, small_off:small_off + SMALL_W]


def _in_proj(x2d, norm_g, w_bf16, tm=512, tn=1152):
    t = x2d.shape[0]
    tm = min(tm, t)
    nj = N_PROJ // tn
    small_off = OFF_SMALL - (nj - 1) * tn
    return pl.pallas_call(
        functools.partial(_inproj_kernel, nj=nj, small_off=small_off),
        grid=(t // tm, nj),
        in_specs=[
            pl.BlockSpec((tm, D_MODEL), lambda i, j: (i, 0)),
            pl.BlockSpec((1, D_MODEL), lambda i, j: (0, 0)),
            pl.BlockSpec((D_MODEL, tn), lambda i, j: (0, j)),
        ],
        out_specs=[
            pl.BlockSpec((tm, tn), lambda i, j: (i, j)),
            pl.BlockSpec((tm, SMALL_W), lambda i, j: (i, 0)),
        ],
        out_shape=[
            jax.ShapeDtypeStruct((t, N_PROJ), BF16),
            jax.ShapeDtypeStruct((t, SMALL_W), F32),
        ],
        scratch_shapes=[pltpu.VMEM((tm, D_MODEL), BF16)],
        compiler_params=_cparams(("parallel", "arbitrary")),
        name="in_proj",
    )(x2d, norm_g, w_bf16)


HALO = 16


def _conv_kernel(prev_ref, main_ref, next_ref, w_ref, out_ref, xp_ref, *, tm, nr):
    r = pl.program_id(1)
    c = pl.program_id(2)
    width = main_ref.shape[1]
    p = prev_ref[...].astype(F32)[HALO - 8:HALO]
    n = next_ref[...].astype(F32)[0:8]
    xp_ref[0:8, :] = jnp.where(r > 0, p, 0.0)
    xp_ref[8:8 + tm, :] = main_ref[...].astype(F32)
    xp_ref[8 + tm:16 + tm, :] = jnp.where(r < nr - 1, n, 0.0)
    w = w_ref[...]
    acc = jnp.zeros((tm, width), F32)
    for j in range(KCONV):
        acc = acc + w[j:j + 1, :] * xp_ref[6 + j:6 + j + tm, :]
    y = _silu(acc)
    norm_on = c < 2
    qscale = jnp.where(c == 0, DK_A ** -0.5, 1.0).astype(F32)
    for h in range(width // LANE):
        seg = y[:, h * LANE:(h + 1) * LANE]
        ss = jnp.sum(seg * seg, axis=-1, keepdims=True)
        scale = jnp.where(norm_on, lax.rsqrt(ss + EPS), 1.0) * qscale
        out_ref[:, h * LANE:(h + 1) * LANE] = (seg * scale).astype(out_ref.dtype)


def _conv_act(proj, conv_w, b, s, tm=512):
    t = b * s
    tm = min(tm, s)
    nr = s // tm
    width = W_A
    hb = tm // HALO

    def prev_map(bi, r, c):
        return (jnp.maximum(bi * (s // HALO) + r * hb - 1, 0), c)

    def next_map(bi, r, c):
        return (jnp.minimum(bi * (s // HALO) + (r + 1) * hb, t // HALO - 1), c)

    return pl.pallas_call(
        functools.partial(_conv_kernel, tm=tm, nr=nr),
        grid=(b, nr, CONV_DIM // width),
        in_specs=[
            pl.BlockSpec((HALO, width), prev_map),
            pl.BlockSpec((tm, width), lambda bi, r, c: (bi * nr + r, c)),
            pl.BlockSpec((HALO, width), next_map),
            pl.BlockSpec((KCONV, width), lambda bi, r, c: (0, c)),
        ],
        out_specs=pl.BlockSpec((tm, width), lambda bi, r, c: (bi * nr + r, c)),
        out_shape=jax.ShapeDtypeStruct((t, CONV_DIM), BF16),
        scratch_shapes=[pltpu.VMEM((tm + 16, width), F32)],
        compiler_params=_cparams(("parallel", "parallel", "parallel")),
        name="conv_act",
    )(proj, proj, proj, conv_w)


N_GATE_BLK = 4


def _gate_kernel(g_ref, alog_ref, dt_ref, cols_ref, rows_ref, *, tm):
    raw = g_ref[...]
    lane = lax.broadcasted_iota(jnp.int32, (CHUNK, LANE), 1)
    z = raw + dt_ref[...]
    softplus = jnp.maximum(z, 0.0) + jnp.log1p(jnp.exp(-jnp.abs(z)))
    g = -jnp.exp(alog_ref[...]) * softplus
    beta = _sigmoid(raw)
    ri = lax.broadcasted_iota(jnp.int32, (CHUNK, CHUNK), 0)
    ci = lax.broadcasted_iota(jnp.int32, (CHUNK, CHUNK), 1)
    lower = (ri >= ci).astype(F32)
    upper = (ri <= ci).astype(F32)
    ones = jnp.ones((CHUNK, CHUNK), F32)
    dot = functools.partial(jnp.dot, precision=lax.Precision.HIGHEST, preferred_element_type=F32)
    for cc in range(tm // CHUNK):
        sl = slice(cc * CHUNK, (cc + 1) * CHUNK)
        gs = g[sl]
        gc = jnp.where(lane < H_A, dot(lower, gs), dot(upper, gs))
        tot = dot(ones, gs)
        blocks = (
            jnp.where(lane < 2 * H_A, gc, beta[sl]),
            jnp.exp(gc),
            jnp.exp(tot - gc),
            jnp.exp(tot),
        )
        for k, blk in enumerate(blocks):
            cols_ref[sl, k * LANE:(k + 1) * LANE] = blk
            rows_ref[k * LANE:(k + 1) * LANE, sl] = blk.T


def _gate_prep(small, alog_row, dt_row, tm=512):
    t = small.shape[0]
    tm = min(tm, t)
    return pl.pallas_call(
        functools.partial(_gate_kernel, tm=tm),
        grid=(t // tm,),
        in_specs=[
            pl.BlockSpec((tm, LANE), lambda i: (i, 1)),
            pl.BlockSpec((1, LANE), lambda i: (0, 0)),
            pl.BlockSpec((1, LANE), lambda i: (0, 0)),
        ],
        out_specs=[
            pl.BlockSpec((tm, N_GATE_BLK * LANE), lambda i: (i, 0)),
            pl.BlockSpec((N_GATE_BLK * LANE, tm), lambda i: (0, i)),
        ],
        out_shape=[
            jax.ShapeDtypeStruct((t, N_GATE_BLK * LANE), F32),
            jax.ShapeDtypeStruct((N_GATE_BLK * LANE, t), F32),
        ],
        compiler_params=_cparams(("parallel",)),
        name="gate_prep",
    )(small, alog_row, dt_row)


def _mm(a, b):
    return jnp.dot(a.astype(BF16), b.astype(BF16), preferred_element_type=F32)


def _delta_kernel(qf_ref, kf_ref, vf_ref, qb_ref, kb_ref, vb_ref, colsf_ref, colsb_ref, rowsf_ref, rowsb_ref,
                  of_ref, ob_ref, s_ref, *, hb):
    t = pl.program_id(2)

    @pl.when(t == 0)
    def _():
        s_ref[...] = jnp.zeros_like(s_ref)

    ri = lax.broadcasted_iota(jnp.int32, (CHUNK, CHUNK), 0)
    ci = lax.broadcasted_iota(jnp.int32, (CHUNK, CHUNK), 1)
    eye = (ri == ci).astype(F32)
    level_masks = [((ri >> (l + 1)) == (ci >> (l + 1))) & ((ri >> l) != (ci >> l))
                   for l in range(int(np.log2(CHUNK)))]
    units = [(backward, hh) for backward in (False, True) for hh in range(hb)]

    def refs(backward):
        return ((qb_ref, kb_ref, vb_ref, colsb_ref, rowsb_ref, ob_ref) if backward
                else (qf_ref, kf_ref, vf_ref, colsf_ref, rowsf_ref, of_ref))

    def col(backward, hh, blk, off=0):
        idx = blk * LANE + off + (H_A if backward else 0) + hh
        return refs(backward)[3][:, idx:idx + 1]

    def row(backward, hh, blk):
        idx = blk * LANE + (H_A if backward else 0) + hh
        return refs(backward)[4][idx:idx + 1, :]

    kbeta, kt, kq = [], [], []
    for backward, hh in units:
        q_ref, k_ref = refs(backward)[:2]
        sl = slice(hh * LANE, (hh + 1) * LANE)
        kf = k_ref[:, sl].astype(F32)
        kbeta.append(kf * col(backward, hh, 0, 2 * H_A))
        kt.append(kf.T)
        kq.append(_mm(jnp.concatenate([kbeta[-1], q_ref[:, sl].astype(F32)], axis=0), kt[-1]))

    attn, lmat, p = [], [], []
    for u, (backward, hh) in enumerate(units):
        tri = (ri <= ci) if backward else (ri >= ci)
        strict = (ri < ci) if backward else (ri > ci)
        diff = col(backward, hh, 0) - row(backward, hh, 0)
        decay = jnp.where(tri, jnp.exp(jnp.where(tri, diff, 0.0)), 0.0)
        attn.append(kq[u][CHUNK:] * decay)
        lmat.append(jnp.where(strict, kq[u][:CHUNK] * decay, 0.0))
        p.append(eye - jnp.where(level_masks[0], lmat[u], 0.0))

    for lm in level_masks[1:]:
        ep = [_mm(jnp.where(lm, lmat[u], 0.0), p[u]) for u in range(len(units))]
        p = [p[u] - _mm(p[u], ep[u]) for u in range(len(units))]

    sol = []
    for u, (backward, hh) in enumerate(units):
        v_ref = refs(backward)[2]
        sl = slice(hh * LANE, (hh + 1) * LANE)
        rhs = jnp.concatenate([v_ref[:, sl].astype(F32) * col(backward, hh, 0, 2 * H_A),
                               kbeta[u] * col(backward, hh, 1)], axis=1)
        sol.append(_mm(p[u], rhs))

    state, wq = [], []
    for u, (backward, hh) in enumerate(units):
        q_ref = refs(backward)[0]
        sl = slice(hh * LANE, (hh + 1) * LANE)
        state.append(s_ref[int(backward), hh])
        qg = q_ref[:, sl].astype(F32) * col(backward, hh, 1)
        wq.append(_mm(jnp.concatenate([sol[u][:, DV_A:], qg], axis=0), state[u]))
    v_new = [sol[u][:, :DV_A] - wq[u][:CHUNK] for u in range(len(units))]
    for u, (backward, hh) in enumerate(units):
        o_ref = refs(backward)[5]
        sl = slice(hh * LANE, (hh + 1) * LANE)
        o_ref[:, sl] = (wq[u][CHUNK:] + _mm(attn[u], v_new[u])).astype(o_ref.dtype)
        s_ref[int(backward), hh] = (state[u] * row(backward, hh, 3)
                                    + _mm(kt[u] * row(backward, hh, 2), v_new[u]))


def _delta(qkv_act, cols, rows, b, s):
    t = b * s
    n = s // CHUNK
    hb = H_A
    nhb = H_A // hb
    wblk = hb * LANE
    qoff, koff, voff = 0, W_A // wblk, 2 * W_A // wblk

    def fwd(off):
        return lambda bi, hi, ti: (bi * n + ti, off + hi)

    def bwd(off):
        return lambda bi, hi, ti: (bi * n + n - 1 - ti, off + hi)

    qkv_specs = [pl.BlockSpec((CHUNK, wblk), m(off)) for m in (fwd, bwd) for off in (qoff, koff, voff)]
    ncol = N_GATE_BLK * LANE
    return pl.pallas_call(
        functools.partial(_delta_kernel, hb=hb),
        grid=(b, nhb, n),
        in_specs=qkv_specs + [
            pl.BlockSpec((CHUNK, ncol), lambda bi, hi, ti: (bi * n + ti, 0)),
            pl.BlockSpec((CHUNK, ncol), lambda bi, hi, ti: (bi * n + n - 1 - ti, 0)),
            pl.BlockSpec((ncol, CHUNK), lambda bi, hi, ti: (0, bi * n + ti)),
            pl.BlockSpec((ncol, CHUNK), lambda bi, hi, ti: (0, bi * n + n - 1 - ti)),
        ],
        out_specs=[pl.BlockSpec((CHUNK, wblk), fwd(0)), pl.BlockSpec((CHUNK, wblk), bwd(0))],
        out_shape=[jax.ShapeDtypeStruct((t, W_A), BF16), jax.ShapeDtypeStruct((t, W_A), BF16)],
        scratch_shapes=[pltpu.VMEM((2, hb, DK_A, DV_A), F32)],
        compiler_params=_cparams(("parallel", "parallel", "arbitrary")),
        name="delta",
    )(qkv_act, qkv_act, qkv_act, qkv_act, qkv_act, qkv_act, cols, cols, rows, rows)


HEAD_PAD = 2 * LANE


def _mla_prep_kernel(qlat_ref, kvlat_ref, rope_ref, cos_ref, sin_ref, gq_ref, gkv_ref,
                     wqn_ref, wqr_ref, wkn_ref, wv_ref, q_out, k_out, v_out):
    scale = (D_NOPE + D_ROPE) ** -0.5
    cosm = cos_ref[...]
    sinm = sin_ref[...]

    def rms(x, g):
        ms = jnp.mean(x * x, axis=-1, keepdims=True)
        return (x * lax.rsqrt(ms + EPS) * g).astype(BF16)

    def rot(blk):
        return blk * cosm + pltpu.roll(blk, LANE // 2, 1) * sinm

    cq = rms(qlat_ref[...].astype(F32), gq_ref[...])
    qn = jnp.dot(cq, wqn_ref[...], preferred_element_type=F32)
    qr = jnp.dot(cq, wqr_ref[...], preferred_element_type=F32)
    ckv = rms(kvlat_ref[...].astype(F32), gkv_ref[...])
    kn = jnp.dot(ckv, wkn_ref[...], preferred_element_type=F32)
    v_out[...] = jnp.dot(ckv, wv_ref[...], preferred_element_type=F32).astype(v_out.dtype)
    kpe = rot(rope_ref[...]).astype(k_out.dtype)
    for h in range(H_B):
        sl = slice(h * LANE, (h + 1) * LANE)
        q_out[:, h * HEAD_PAD:h * HEAD_PAD + LANE] = (qn[:, sl] * scale).astype(q_out.dtype)
        q_out[:, h * HEAD_PAD + LANE:(h + 1) * HEAD_PAD] = (rot(qr[:, sl]) * scale).astype(q_out.dtype)
        k_out[:, h * HEAD_PAD:h * HEAD_PAD + LANE] = kn[:, sl].astype(k_out.dtype)
        k_out[:, h * HEAD_PAD + LANE:(h + 1) * HEAD_PAD] = kpe


def _mla_prep(proj, small, cosm, sinm, gq, gkv, wqn, wqr, wkn, wv, b, s, tm=512):
    t = b * s
    tm = min(tm, s)
    nr = s // tm
    row = lambda bi, r: bi * nr + r
    const = lambda bi, r: (0, 0)
    return pl.pallas_call(
        _mla_prep_kernel,
        grid=(b, nr),
        in_specs=[
            pl.BlockSpec((tm, Q_LORA), lambda bi, r: (row(bi, r), OFF_QLAT // Q_LORA)),
            pl.BlockSpec((tm, KV_LORA), lambda bi, r: (row(bi, r), OFF_KVLAT // KV_LORA)),
            pl.BlockSpec((tm, LANE), lambda bi, r: (row(bi, r), 0)),
            pl.BlockSpec((tm, LANE), lambda bi, r: (r, 0)),
            pl.BlockSpec((tm, LANE), lambda bi, r: (r, 0)),
            pl.BlockSpec((1, Q_LORA), const),
            pl.BlockSpec((1, KV_LORA), const),
            pl.BlockSpec((Q_LORA, W_B), const),
            pl.BlockSpec((Q_LORA, W_B), const),
            pl.BlockSpec((KV_LORA, W_B), const),
            pl.BlockSpec((KV_LORA, W_B), const),
        ],
        out_specs=[
            pl.BlockSpec((tm, H_B * HEAD_PAD), lambda bi, r: (row(bi, r), 0)),
            pl.BlockSpec((tm, H_B * HEAD_PAD), lambda bi, r: (row(bi, r), 0)),
            pl.BlockSpec((tm, W_B), lambda bi, r: (row(bi, r), 0)),
        ],
        out_shape=[
            jax.ShapeDtypeStruct((t, H_B * HEAD_PAD), BF16),
            jax.ShapeDtypeStruct((t, H_B * HEAD_PAD), BF16),
            jax.ShapeDtypeStruct((t, W_B), BF16),
        ],
        compiler_params=_cparams(("parallel", "parallel")),
        name="mla_prep",
    )(proj, proj, small, cosm, sinm, gq, gkv, wqn, wqr, wkn, wv)


def _attn_kernel(q_ref, k_ref, v_ref, o_ref, m_ref, l_ref, acc_ref, *, nk):
    j = pl.program_id(3)

    @pl.when(j == 0)
    def _():
        m_ref[...] = jnp.full_like(m_ref, -jnp.inf)
        l_ref[...] = jnp.zeros_like(l_ref)
        acc_ref[...] = jnp.zeros_like(acc_ref)

    sc = lax.dot_general(q_ref[...], k_ref[...], (((1,), (1,)), ((), ())), preferred_element_type=F32)
    m_prev = m_ref[...]
    m_new = jnp.maximum(m_prev, jnp.max(sc, axis=-1, keepdims=True))
    alpha = jnp.exp(m_prev - m_new)
    p = jnp.exp(sc - m_new)
    l_ref[...] = alpha * l_ref[...] + jnp.sum(p, axis=-1, keepdims=True)
    acc_ref[...] = alpha * acc_ref[...] + jnp.dot(p.astype(BF16), v_ref[...], preferred_element_type=F32)
    m_ref[...] = m_new

    @pl.when(j == nk - 1)
    def _():
        o_ref[...] = (acc_ref[...] / l_ref[...]).astype(o_ref.dtype)


def _attention(qp, kp, vp, b, s, tq=1024, tk=1024):
    t = b * s
    tq = min(tq, s)
    tk = min(tk, s)
    nq, nk = s // tq, s // tk
    return pl.pallas_call(
        functools.partial(_attn_kernel, nk=nk),
        grid=(b, H_B, nq, nk),
        in_specs=[
            pl.BlockSpec((tq, HEAD_PAD), lambda bi, h, i, j: (bi * nq + i, h)),
            pl.BlockSpec((tk, HEAD_PAD), lambda bi, h, i, j: (bi * nk + j, h)),
            pl.BlockSpec((tk, DV_B), lambda bi, h, i, j: (bi * nk + j, h)),
        ],
        out_specs=pl.BlockSpec((tq, DV_B), lambda bi, h, i, j: (bi * nq + i, h)),
        out_shape=jax.ShapeDtypeStruct((t, W_B), BF16),
        scratch_shapes=[pltpu.VMEM((tq, 1), F32), pltpu.VMEM((tq, 1), F32), pltpu.VMEM((tq, DV_B), F32)],
        compiler_params=_cparams(("parallel", "parallel", "parallel", "arbitrary")),
        name="attention",
    )(qp, kp, vp)


def _merge_kernel(of_ref, ob_ref, ga_ref, attn_ref, gb_ref, gma_ref, gmb_ref, x_ref,
                  onorm_ref, nf_ref, wpa_ref, wpb_ref, wout_ref, y_ref):
    oa = of_ref[...].astype(F32) + ob_ref[...].astype(F32)
    gate_a = _silu(ga_ref[...].astype(F32))
    onorm = onorm_ref[...]
    parts = []
    for h in range(H_A):
        sl = slice(h * DV_A, (h + 1) * DV_A)
        seg = oa[:, sl]
        ms = jnp.mean(seg * seg, axis=-1, keepdims=True)
        parts.append((seg * lax.rsqrt(ms + EPS) * onorm * gate_a[:, sl]).astype(BF16))
    o_a = jnp.concatenate(parts, axis=-1)
    o_b = (attn_ref[...].astype(F32) * _silu(gb_ref[...].astype(F32))).astype(BF16)
    pa = jnp.dot(o_a, wpa_ref[...], preferred_element_type=F32)
    pb = jnp.dot(o_b, wpb_ref[...], preferred_element_type=F32)
    m = _sigmoid(gma_ref[...].astype(F32)) * pa + _sigmoid(gmb_ref[...].astype(F32)) * pb
    y = x_ref[...] + jnp.dot(m.astype(BF16), wout_ref[...], preferred_element_type=F32)
    ms = jnp.mean(y * y, axis=-1, keepdims=True)
    y_ref[...] = y * lax.rsqrt(ms + EPS) * nf_ref[...]


def _merge_out(o_f, o_b, proj, attn, x2d, onorm, nf, wpa, wpb, wout, tm=256):
    t = x2d.shape[0]
    tm = min(tm, t)
    const = lambda i: (0, 0)
    single = pl.Buffered(1)
    return pl.pallas_call(
        _merge_kernel,
        grid=(t // tm,),
        in_specs=[
            pl.BlockSpec((tm, W_A), lambda i: (i, 0)),
            pl.BlockSpec((tm, W_A), lambda i: (i, 0)),
            pl.BlockSpec((tm, W_A), lambda i: (i, OFF_GATE_A // W_A)),
            pl.BlockSpec((tm, W_B), lambda i: (i, 0)),
            pl.BlockSpec((tm, W_B), lambda i: (i, OFF_GATE_B // W_B)),
            pl.BlockSpec((tm, D_MODEL), lambda i: (i, OFF_GM_A // D_MODEL)),
            pl.BlockSpec((tm, D_MODEL), lambda i: (i, OFF_GM_B // D_MODEL)),
            pl.BlockSpec((tm, D_MODEL), lambda i: (i, 0)),
            pl.BlockSpec((1, DV_A), const),
            pl.BlockSpec((1, D_MODEL), const),
            pl.BlockSpec((W_A, D_MODEL), const, pipeline_mode=single),
            pl.BlockSpec((W_B, D_MODEL), const, pipeline_mode=single),
            pl.BlockSpec((D_MODEL, D_MODEL), const, pipeline_mode=single),
        ],
        out_specs=pl.BlockSpec((tm, D_MODEL), lambda i: (i, 0)),
        out_shape=jax.ShapeDtypeStruct((t, D_MODEL), F32),
        compiler_params=_cparams(("parallel",)),
        name="merge_out",
    )(o_f, o_b, proj, attn, proj, proj, proj, x2d, onorm, nf, wpa, wpb, wout)


def _prep_weights(norm_in, w_in, conv_w, a_log_f, dt_bias_f, a_log_b, dt_bias_b, o_norm_a,
                  q_a_norm, w_q_b, kv_a_norm, w_kv_b, w_pa, w_pb, w_out, norm_f):
    w = w_in
    o = np.cumsum((0, CONV_DIM, W_A, H_A, H_A, H_A, H_A, Q_LORA, KV_LORA, D_ROPE, W_B, D_MODEL, D_MODEL))
    seg = lambda i: w[:, int(o[i]):int(o[i + 1])]
    qkv, gate_a, a_f, a_b, b_f, b_b, q_lat, kv_lat, k_rope, gate_b, gm_a, gm_b = (seg(i) for i in range(12))
    half = D_ROPE // 2
    k_rope_sw = jnp.concatenate([k_rope[:, half:], k_rope[:, :half]], axis=1)
    pad = jnp.zeros((D_MODEL, LANE - 4 * H_A), w.dtype)
    w_proj = jnp.concatenate([qkv, gate_a, gm_a, gm_b, gate_b, q_lat, kv_lat, k_rope, k_rope_sw,
                              a_f, a_b, b_f, b_b, pad], axis=1).astype(BF16)
    lane_pad = lambda f, bk: jnp.concatenate([f, bk, jnp.zeros((LANE - 2 * H_A,), F32)])[None, :]
    wq = w_q_b.reshape(Q_LORA, H_B, D_NOPE + D_ROPE)
    wq_rope = wq[:, :, D_NOPE:]
    wq_rope_sw = jnp.concatenate([wq_rope[..., half:], wq_rope[..., :half]], axis=-1)
    wkv = w_kv_b.reshape(KV_LORA, H_B, D_NOPE + DV_B)
    return dict(
        norm_in=norm_in[None, :], w_proj=w_proj, conv_w=conv_w,
        alog=lane_pad(a_log_f, a_log_b), dt=lane_pad(dt_bias_f, dt_bias_b),
        onorm=o_norm_a[None, :], nf=norm_f[None, :],
        gq=q_a_norm[None, :], gkv=kv_a_norm[None, :],
        wqn=wq[:, :, :D_NOPE].reshape(Q_LORA, W_B).astype(BF16),
        wqr=jnp.concatenate([wq_rope, wq_rope_sw], axis=-1).reshape(Q_LORA, W_B).astype(BF16),
        wkn=wkv[:, :, :D_NOPE].reshape(KV_LORA, W_B).astype(BF16),
        wv=wkv[:, :, D_NOPE:].reshape(KV_LORA, W_B).astype(BF16),
        wpa=w_pa.astype(BF16), wpb=w_pb.astype(BF16), wout=w_out.astype(BF16),
    )


def _rope_tables(s):
    pos = jnp.arange(s, dtype=F32)
    inv_freq = ROPE_THETA ** (-jnp.arange(0, D_ROPE, 2, dtype=F32) / D_ROPE)
    ang = pos[:, None] * inv_freq[None, :]
    cos, sin = jnp.cos(ang), jnp.sin(ang)
    z = jnp.zeros((s, LANE // 2), F32)
    return jnp.concatenate([cos, cos, z], axis=1), jnp.concatenate([-sin, sin, z], axis=1)


def _trunk(x, p):
    b, s, _ = x.shape
    x2d = x.reshape(b * s, D_MODEL)
    proj, small = _in_proj(x2d, p["norm_in"], p["w_proj"])
    qkv_act = _conv_act(proj, p["conv_w"], b, s)
    cols, rows = _gate_prep(small, p["alog"], p["dt"])
    o_f, o_b = _delta(qkv_act, cols, rows, b, s)
    cosm, sinm = _rope_tables(s)
    qp, kp, vp = _mla_prep(proj, small, cosm, sinm, p["gq"], p["gkv"], p["wqn"], p["wqr"], p["wkn"], p["wv"], b, s)
    attn = _attention(qp, kp, vp, b, s)
    y = _merge_out(o_f, o_b, proj, attn, x2d, p["onorm"], p["nf"], p["wpa"], p["wpb"], p["wout"])
    return y.reshape(b, s, D_MODEL)


def kernel(x_prompt, x_sample, norm_in, w_in, conv_w, a_log_f, dt_bias_f, a_log_b, dt_bias_b, o_norm_a,
           q_a_norm, w_q_b, kv_a_norm, w_kv_b, w_pa, w_pb, w_out, norm_f):
    assert norm_in.shape[0] == 1, "single-layer trunk"
    p = _prep_weights(norm_in[0], w_in[0], conv_w[0], a_log_f[0], dt_bias_f[0], a_log_b[0], dt_bias_b[0],
                      o_norm_a[0], q_a_norm[0], w_q_b[0], kv_a_norm[0], w_kv_b[0], w_pa[0], w_pb[0], w_out[0],
                      norm_f)
    return (_trunk(x_prompt, p), _trunk(x_sample, p))
```

```python
import functools

import jax
import jax.numpy as jnp
import numpy as np
from jax import lax
from jax.experimental import pallas as pl
from jax.experimental.pallas import tpu as pltpu

F32 = jnp.float32
BF16 = jnp.bfloat16

D_MODEL = 2048
H_A = 8
DK_A = 128
DV_A = 128
W_A = H_A * DV_A
CONV_DIM = 2 * H_A * DK_A + H_A * DV_A
KCONV = 5
H_B = 8
Q_LORA = 1536
KV_LORA = 512
D_NOPE = 128
D_ROPE = 64
DV_B = 128
W_B = H_B * DV_B
ROPE_THETA = 10000.0
EPS = 1e-6

LANE = 128
CHUNK = 128

OFF_QKV = 0
OFF_GATE_A = 3072
OFF_GM_A = 4096
OFF_GM_B = 6144
OFF_GATE_B = 8192
OFF_QLAT = 9216
OFF_KVLAT = 10752
OFF_SMALL = 11264
N_PROJ = 11520
SMALL_W = 256

VMEM_LIMIT = 56 * 1024 * 1024


def _cparams(sem):
    return pltpu.CompilerParams(dimension_semantics=sem, vmem_limit_bytes=VMEM_LIMIT)


def _sigmoid(x):
    return 1.0 / (1.0 + jnp.exp(-x))


def _silu(x):
    return x * _sigmoid(x)


def _inproj_kernel(x_ref, g_ref, w_ref, out_ref, small_ref, xn_ref, *, nj, small_off):
    j = pl.program_id(1)

    @pl.when(j == 0)
    def _():
        xf = x_ref[...]
        ms = jnp.mean(xf * xf, axis=-1, keepdims=True)
        xn_ref[...] = (xf * lax.rsqrt(ms + EPS) * g_ref[...]).astype(BF16)

    acc = jnp.dot(xn_ref[...], w_ref[...], preferred_element_type=F32)
    out_ref[...] = acc.astype(out_ref.dtype)

    @pl.when(j == nj - 1)
    def _():
        small_ref[...] = acc[:, small_off:small_off + SMALL_W]


def _in_proj(x2d, norm_g, w_bf16, tm=1024, tn=1152):
    t = x2d.shape[0]
    tm = min(tm, t)
    nj = N_PROJ // tn
    small_off = OFF_SMALL - (nj - 1) * tn
    return pl.pallas_call(
        functools.partial(_inproj_kernel, nj=nj, small_off=small_off),
        grid=(t // tm, nj),
        in_specs=[
            pl.BlockSpec((tm, D_MODEL), lambda i, j: (i, 0)),
            pl.BlockSpec((1, D_MODEL), lambda i, j: (0, 0)),
            pl.BlockSpec((D_MODEL, tn), lambda i, j: (0, j)),
        ],
        out_specs=[
            pl.BlockSpec((tm, tn), lambda i, j: (i, j)),
            pl.BlockSpec((tm, SMALL_W), lambda i, j: (i, 0)),
        ],
        out_shape=[
            jax.ShapeDtypeStruct((t, N_PROJ), BF16),
            jax.ShapeDtypeStruct((t, SMALL_W), F32),
        ],
        scratch_shapes=[pltpu.VMEM((tm, D_MODEL), BF16)],
        compiler_params=_cparams(("parallel", "arbitrary")),
        name="in_proj",
    )(x2d, norm_g, w_bf16)


HALO = 16


def _conv_kernel(prev_ref, main_ref, next_ref, w_ref, out_ref, xp_ref, *, tm, nr):
    r = pl.program_id(1)
    c = pl.program_id(2)
    width = main_ref.shape[1]
    p = prev_ref[...].astype(F32)[HALO - 8:HALO]
    n = next_ref[...].astype(F32)[0:8]
    xp_ref[0:8, :] = jnp.where(r > 0, p, 0.0)
    xp_ref[8:8 + tm, :] = main_ref[...].astype(F32)
    xp_ref[8 + tm:16 + tm, :] = jnp.where(r < nr - 1, n, 0.0)
    w = w_ref[...]
    acc = w[0:1, :] * xp_ref[6:6 + tm, :]
    for j in range(1, KCONV):
        acc = acc + w[j:j + 1, :] * xp_ref[6 + j:6 + j + tm, :]
    y = _silu(acc)

    @pl.when(c < 2)
    def _():
        qscale = jnp.where(c == 0, DK_A ** -0.5, 1.0).astype(F32)
        for h in range(width // LANE):
            seg = y[:, h * LANE:(h + 1) * LANE]
            ss = jnp.sum(seg * seg, axis=-1, keepdims=True)
            out_ref[:, h * LANE:(h + 1) * LANE] = (seg * (lax.rsqrt(ss + EPS) * qscale)).astype(out_ref.dtype)

    @pl.when(c == 2)
    def _():
        out_ref[...] = y.astype(out_ref.dtype)


def _conv_act(proj, conv_w, b, s, tm=512):
    t = b * s
    tm = min(tm, s)
    nr = s // tm
    width = W_A
    hb = tm // HALO

    def prev_map(bi, r, c):
        return (jnp.maximum(bi * (s // HALO) + r * hb - 1, 0), c)

    def next_map(bi, r, c):
        return (jnp.minimum(bi * (s // HALO) + (r + 1) * hb, t // HALO - 1), c)

    return pl.pallas_call(
        functools.partial(_conv_kernel, tm=tm, nr=nr),
        grid=(b, nr, CONV_DIM // width),
        in_specs=[
            pl.BlockSpec((HALO, width), prev_map),
            pl.BlockSpec((tm, width), lambda bi, r, c: (bi * nr + r, c)),
            pl.BlockSpec((HALO, width), next_map),
            pl.BlockSpec((KCONV, width), lambda bi, r, c: (0, c)),
        ],
        out_specs=pl.BlockSpec((tm, width), lambda bi, r, c: (bi * nr + r, c)),
        out_shape=jax.ShapeDtypeStruct((t, CONV_DIM), BF16),
        scratch_shapes=[pltpu.VMEM((tm + 16, width), F32)],
        compiler_params=_cparams(("parallel", "parallel", "parallel")),
        name="conv_act",
    )(proj, proj, proj, conv_w)


N_GATE_BLK = 4


def _gate_kernel(g_ref, alog_ref, dt_ref, cols_ref, rows_ref, *, tm):
    raw = g_ref[...]
    lane = lax.broadcasted_iota(jnp.int32, (CHUNK, LANE), 1)
    z = raw + dt_ref[...]
    softplus = jnp.maximum(z, 0.0) + jnp.log1p(jnp.exp(-jnp.abs(z)))
    g = -jnp.exp(alog_ref[...]) * softplus
    beta = _sigmoid(raw)
    ri = lax.broadcasted_iota(jnp.int32, (CHUNK, CHUNK), 0)
    ci = lax.broadcasted_iota(jnp.int32, (CHUNK, CHUNK), 1)
    lower = (ri >= ci).astype(F32)
    upper = (ri <= ci).astype(F32)
    ones = jnp.ones((CHUNK, CHUNK), F32)
    dot = functools.partial(jnp.dot, precision=lax.Precision.HIGHEST, preferred_element_type=F32)
    for cc in range(tm // CHUNK):
        sl = slice(cc * CHUNK, (cc + 1) * CHUNK)
        gs = g[sl]
        gc = jnp.where(lane < H_A, dot(lower, gs), dot(upper, gs))
        tot = dot(ones, gs)
        blocks = (
            jnp.where(lane < 2 * H_A, gc, beta[sl]),
            jnp.exp(gc),
            jnp.exp(tot - gc),
            jnp.exp(tot),
        )
        for k, blk in enumerate(blocks):
            cols_ref[sl, k * LANE:(k + 1) * LANE] = blk
            rows_ref[k * LANE:(k + 1) * LANE, sl] = blk.T


def _gate_prep(small, alog_row, dt_row, tm=512):
    t = small.shape[0]
    tm = min(tm, t)
    return pl.pallas_call(
        functools.partial(_gate_kernel, tm=tm),
        grid=(t // tm,),
        in_specs=[
            pl.BlockSpec((tm, LANE), lambda i: (i, 1)),
            pl.BlockSpec((1, LANE), lambda i: (0, 0)),
            pl.BlockSpec((1, LANE), lambda i: (0, 0)),
        ],
        out_specs=[
            pl.BlockSpec((tm, N_GATE_BLK * LANE), lambda i: (i, 0)),
            pl.BlockSpec((N_GATE_BLK * LANE, tm), lambda i: (0, i)),
        ],
        out_shape=[
            jax.ShapeDtypeStruct((t, N_GATE_BLK * LANE), F32),
            jax.ShapeDtypeStruct((N_GATE_BLK * LANE, t), F32),
        ],
        compiler_params=_cparams(("parallel",)),
        name="gate_prep",
    )(small, alog_row, dt_row)


def _mm(a, b):
    return jnp.dot(a.astype(BF16), b.astype(BF16), preferred_element_type=F32)


def _delta_kernel(qf_ref, kf_ref, vf_ref, qb_ref, kb_ref, vb_ref, colsf_ref, colsb_ref, rowsf_ref, rowsb_ref,
                  of_ref, ob_ref, s_ref, *, hb):
    t = pl.program_id(2)

    @pl.when(t == 0)
    def _():
        s_ref[...] = jnp.zeros_like(s_ref)

    ri = lax.broadcasted_iota(jnp.int32, (CHUNK, CHUNK), 0)
    ci = lax.broadcasted_iota(jnp.int32, (CHUNK, CHUNK), 1)
    eye = (ri == ci).astype(F32)
    level_masks = [((ri >> (l + 1)) == (ci >> (l + 1))) & ((ri >> l) != (ci >> l))
                   for l in range(int(np.log2(CHUNK)))]
    units = [(backward, hh) for backward in (False, True) for hh in range(hb)]

    def refs(backward):
        return ((qb_ref, kb_ref, vb_ref, colsb_ref, rowsb_ref, ob_ref) if backward
                else (qf_ref, kf_ref, vf_ref, colsf_ref, rowsf_ref, of_ref))

    def col(backward, hh, blk, off=0):
        idx = blk * LANE + off + (H_A if backward else 0) + hh
        return refs(backward)[3][:, idx:idx + 1]

    def row(backward, hh, blk):
        idx = blk * LANE + (H_A if backward else 0) + hh
        return refs(backward)[4][idx:idx + 1, :]

    kbeta, kt, kq = [], [], []
    for backward, hh in units:
        q_ref, k_ref = refs(backward)[:2]
        sl = slice(hh * LANE, (hh + 1) * LANE)
        kf = k_ref[:, sl].astype(F32)
        kbeta.append(kf * col(backward, hh, 0, 2 * H_A))
        kt.append(kf.T)
        kq.append(_mm(jnp.concatenate([kbeta[-1], q_ref[:, sl].astype(F32)], axis=0), kt[-1]))

    attn, lmat, p = [], [], []
    for u, (backward, hh) in enumerate(units):
        tri = (ri <= ci) if backward else (ri >= ci)
        strict = (ri < ci) if backward else (ri > ci)
        diff = col(backward, hh, 0) - row(backward, hh, 0)
        decay = jnp.where(tri, jnp.exp(jnp.where(tri, diff, 0.0)), 0.0)
        attn.append(kq[u][CHUNK:] * decay)
        lmat.append(jnp.where(strict, kq[u][:CHUNK] * decay, 0.0))
        p.append(eye - jnp.where(level_masks[0], lmat[u], 0.0))

    for lm in level_masks[1:]:
        ep = [_mm(jnp.where(lm, lmat[u], 0.0), p[u]) for u in range(len(units))]
        p = [p[u] - _mm(p[u], ep[u]) for u in range(len(units))]

    sol = []
    for u, (backward, hh) in enumerate(units):
        v_ref = refs(backward)[2]
        sl = slice(hh * LANE, (hh + 1) * LANE)
        rhs = jnp.concatenate([v_ref[:, sl].astype(F32) * col(backward, hh, 0, 2 * H_A),
                               kbeta[u] * col(backward, hh, 1)], axis=1)
        sol.append(_mm(p[u], rhs))

    state, wq = [], []
    for u, (backward, hh) in enumerate(units):
        q_ref = refs(backward)[0]
        sl = slice(hh * LANE, (hh + 1) * LANE)
        state.append(s_ref[int(backward), hh])
        qg = q_ref[:, sl].astype(F32) * col(backward, hh, 1)
        wq.append(_mm(jnp.concatenate([sol[u][:, DV_A:], qg], axis=0), state[u]))
    v_new = [sol[u][:, :DV_A] - wq[u][:CHUNK] for u in range(len(units))]
    for u, (backward, hh) in enumerate(units):
        o_ref = refs(backward)[5]
        sl = slice(hh * LANE, (hh + 1) * LANE)
        o_ref[:, sl] = (wq[u][CHUNK:] + _mm(attn[u], v_new[u])).astype(o_ref.dtype)
        s_ref[int(backward), hh] = (state[u] * row(backward, hh, 3)
                                    + _mm(kt[u] * row(backward, hh, 2), v_new[u]))


def _delta(qkv_act, cols, rows, b, s):
    t = b * s
    n = s // CHUNK
    hb = H_A
    nhb = H_A // hb
    wblk = hb * LANE
    qoff, koff, voff = 0, W_A // wblk, 2 * W_A // wblk

    def fwd(off):
        return lambda bi, hi, ti: (bi * n + ti, off + hi)

    def bwd(off):
        return lambda bi, hi, ti: (bi * n + n - 1 - ti, off + hi)

    qkv_specs = [pl.BlockSpec((CHUNK, wblk), m(off)) for m in (fwd, bwd) for off in (qoff, koff, voff)]
    ncol = N_GATE_BLK * LANE
    return pl.pallas_call(
        functools.partial(_delta_kernel, hb=hb),
        grid=(b, nhb, n),
        in_specs=qkv_specs + [
            pl.BlockSpec((CHUNK, ncol), lambda bi, hi, ti: (bi * n + ti, 0)),
            pl.BlockSpec((CHUNK, ncol), lambda bi, hi, ti: (bi * n + n - 1 - ti, 0)),
            pl.BlockSpec((ncol, CHUNK), lambda bi, hi, ti: (0, bi * n + ti)),
            pl.BlockSpec((ncol, CHUNK), lambda bi, hi, ti: (0, bi * n + n - 1 - ti)),
        ],
        out_specs=[pl.BlockSpec((CHUNK, wblk), fwd(0)), pl.BlockSpec((CHUNK, wblk), bwd(0))],
        out_shape=[jax.ShapeDtypeStruct((t, W_A), BF16), jax.ShapeDtypeStruct((t, W_A), BF16)],
        scratch_shapes=[pltpu.VMEM((2, hb, DK_A, DV_A), F32)],
        compiler_params=_cparams(("parallel", "parallel", "arbitrary")),
        name="delta",
    )(qkv_act, qkv_act, qkv_act, qkv_act, qkv_act, qkv_act, cols, cols, rows, rows)


HEAD_PAD = 2 * LANE


def _mla_prep_kernel(qlat_ref, kvlat_ref, rope_ref, cos_ref, sin_ref, gq_ref, gkv_ref,
                     wqn_ref, wqr_ref, wkn_ref, wv_ref, q_out, k_out, v_out):
    scale = (D_NOPE + D_ROPE) ** -0.5
    cosm = cos_ref[...]
    sinm = sin_ref[...]

    def rms(x, g):
        ms = jnp.mean(x * x, axis=-1, keepdims=True)
        return (x * lax.rsqrt(ms + EPS) * g).astype(BF16)

    def rot(blk):
        return blk * cosm + pltpu.roll(blk, LANE // 2, 1) * sinm

    cq = rms(qlat_ref[...].astype(F32), gq_ref[...])
    qn = jnp.dot(cq, wqn_ref[...], preferred_element_type=F32)
    qr = jnp.dot(cq, wqr_ref[...], preferred_element_type=F32)
    ckv = rms(kvlat_ref[...].astype(F32), gkv_ref[...])
    kn = jnp.dot(ckv, wkn_ref[...], preferred_element_type=F32)
    v_out[...] = jnp.dot(ckv, wv_ref[...], preferred_element_type=F32).astype(v_out.dtype)
    kpe = rot(rope_ref[...]).astype(k_out.dtype)
    for h in range(H_B):
        sl = slice(h * LANE, (h + 1) * LANE)
        q_out[:, h * HEAD_PAD:h * HEAD_PAD + LANE] = (qn[:, sl] * scale).astype(q_out.dtype)
        q_out[:, h * HEAD_PAD + LANE:(h + 1) * HEAD_PAD] = (rot(qr[:, sl]) * scale).astype(q_out.dtype)
        k_out[:, h * HEAD_PAD:h * HEAD_PAD + LANE] = kn[:, sl].astype(k_out.dtype)
        k_out[:, h * HEAD_PAD + LANE:(h + 1) * HEAD_PAD] = kpe


def _mla_prep(proj, small, cosm, sinm, gq, gkv, wqn, wqr, wkn, wv, b, s, tm=512):
    t = b * s
    tm = min(tm, s)
    nr = s // tm
    row = lambda bi, r: bi * nr + r
    const = lambda bi, r: (0, 0)
    return pl.pallas_call(
        _mla_prep_kernel,
        grid=(b, nr),
        in_specs=[
            pl.BlockSpec((tm, Q_LORA), lambda bi, r: (row(bi, r), OFF_QLAT // Q_LORA)),
            pl.BlockSpec((tm, KV_LORA), lambda bi, r: (row(bi, r), OFF_KVLAT // KV_LORA)),
            pl.BlockSpec((tm, LANE), lambda bi, r: (row(bi, r), 0)),
            pl.BlockSpec((tm, LANE), lambda bi, r: (r, 0)),
            pl.BlockSpec((tm, LANE), lambda bi, r: (r, 0)),
            pl.BlockSpec((1, Q_LORA), const),
            pl.BlockSpec((1, KV_LORA), const),
            pl.BlockSpec((Q_LORA, W_B), const),
            pl.BlockSpec((Q_LORA, W_B), const),
            pl.BlockSpec((KV_LORA, W_B), const),
            pl.BlockSpec((KV_LORA, W_B), const),
        ],
        out_specs=[
            pl.BlockSpec((tm, H_B * HEAD_PAD), lambda bi, r: (row(bi, r), 0)),
            pl.BlockSpec((tm, H_B * HEAD_PAD), lambda bi, r: (row(bi, r), 0)),
            pl.BlockSpec((tm, W_B), lambda bi, r: (row(bi, r), 0)),
        ],
        out_shape=[
            jax.ShapeDtypeStruct((t, H_B * HEAD_PAD), BF16),
            jax.ShapeDtypeStruct((t, H_B * HEAD_PAD), BF16),
            jax.ShapeDtypeStruct((t, W_B), BF16),
        ],
        compiler_params=_cparams(("parallel", "parallel")),
        name="mla_prep",
    )(proj, proj, small, cosm, sinm, gq, gkv, wqn, wqr, wkn, wv)


def _attn_kernel(q_ref, k_ref, v_ref, o_ref, m_ref, acc_ref, *, nk, nsub):
    j = pl.program_id(3)

    @pl.when(j == 0)
    def _():
        m_ref[...] = jnp.full_like(m_ref, -jnp.inf)
        acc_ref[...] = jnp.zeros_like(acc_ref)

    rb = q_ref.shape[0] // nsub
    rows = [slice(r * rb, (r + 1) * rb) for r in range(nsub)]
    k = k_ref[...]
    v = v_ref[...]
    v_ext = jnp.concatenate([v, jnp.ones_like(v)], axis=1)
    sc = [lax.dot_general(q_ref[rs, :], k, (((1,), (1,)), ((), ())), preferred_element_type=F32) for rs in rows]
    m_prev = [m_ref[rs, :] for rs in rows]
    m_new = [jnp.maximum(mp, jnp.max(s_, axis=-1, keepdims=True)) for mp, s_ in zip(m_prev, sc)]
    p = [jnp.exp(s_ - mn).astype(BF16) for s_, mn in zip(sc, m_new)]
    pv = [jnp.dot(p_, v_ext, preferred_element_type=F32) for p_ in p]
    for rs, mp, mn, pv_ in zip(rows, m_prev, m_new, pv):
        acc_ref[rs, :] = jnp.exp(mp - mn) * acc_ref[rs, :] + pv_
        m_ref[rs, :] = mn

    @pl.when(j == nk - 1)
    def _():
        acc = acc_ref[...]
        o_ref[...] = (acc[:, :DV_B] / acc[:, DV_B:]).astype(o_ref.dtype)


def _attention(qp, kp, vp, b, s, tq=2048, tk=1024, nsub=8):
    t = b * s
    tq = min(tq, s)
    tk = min(tk, s)
    nq, nk = s // tq, s // tk
    return pl.pallas_call(
        functools.partial(_attn_kernel, nk=nk, nsub=nsub),
        grid=(b, H_B, nq, nk),
        in_specs=[
            pl.BlockSpec((tq, HEAD_PAD), lambda bi, h, i, j: (bi * nq + i, h)),
            pl.BlockSpec((tk, HEAD_PAD), lambda bi, h, i, j: (bi * nk + j, h)),
            pl.BlockSpec((tk, DV_B), lambda bi, h, i, j: (bi * nk + j, h)),
        ],
        out_specs=pl.BlockSpec((tq, DV_B), lambda bi, h, i, j: (bi * nq + i, h)),
        out_shape=jax.ShapeDtypeStruct((t, W_B), BF16),
        scratch_shapes=[pltpu.VMEM((tq, 1), F32), pltpu.VMEM((tq, 2 * DV_B), F32)],
        compiler_params=_cparams(("parallel", "parallel", "parallel", "arbitrary")),
        name="attention",
    )(qp, kp, vp)


def _merge_kernel(of_ref, ob_ref, ga_ref, attn_ref, gb_ref, gma_ref, gmb_ref, x_ref,
                  onorm_ref, nf_ref, wpa_ref, wpb_ref, wout_ref, y_ref):
    oa = of_ref[...].astype(F32) + ob_ref[...].astype(F32)
    gate_a = _silu(ga_ref[...].astype(F32))
    onorm = onorm_ref[...]
    parts = []
    for h in range(H_A):
        sl = slice(h * DV_A, (h + 1) * DV_A)
        seg = oa[:, sl]
        ms = jnp.mean(seg * seg, axis=-1, keepdims=True)
        parts.append((seg * lax.rsqrt(ms + EPS) * onorm * gate_a[:, sl]).astype(BF16))
    o_a = jnp.concatenate(parts, axis=-1)
    o_b = (attn_ref[...].astype(F32) * _silu(gb_ref[...].astype(F32))).astype(BF16)
    pa = jnp.dot(o_a, wpa_ref[...], preferred_element_type=F32)
    pb = jnp.dot(o_b, wpb_ref[...], preferred_element_type=F32)
    m = _sigmoid(gma_ref[...].astype(F32)) * pa + _sigmoid(gmb_ref[...].astype(F32)) * pb
    y = x_ref[...] + jnp.dot(m.astype(BF16), wout_ref[...], preferred_element_type=F32)
    ms = jnp.mean(y * y, axis=-1, keepdims=True)
    y_ref[...] = y * lax.rsqrt(ms + EPS) * nf_ref[...]


def _merge_out(o_f, o_b, proj, attn, x2d, onorm, nf, wpa, wpb, wout, tm=512):
    t = x2d.shape[0]
    tm = min(tm, t)
    const = lambda i: (0, 0)
    single = pl.Buffered(1)
    return pl.pallas_call(
        _merge_kernel,
        grid=(t // tm,),
        in_specs=[
            pl.BlockSpec((tm, W_A), lambda i: (i, 0)),
            pl.BlockSpec((tm, W_A), lambda i: (i, 0)),
            pl.BlockSpec((tm, W_A), lambda i: (i, OFF_GATE_A // W_A)),
            pl.BlockSpec((tm, W_B), lambda i: (i, 0)),
            pl.BlockSpec((tm, W_B), lambda i: (i, OFF_GATE_B // W_B)),
            pl.BlockSpec((tm, D_MODEL), lambda i: (i, OFF_GM_A // D_MODEL)),
            pl.BlockSpec((tm, D_MODEL), lambda i: (i, OFF_GM_B // D_MODEL)),
            pl.BlockSpec((tm, D_MODEL), lambda i: (i, 0)),
            pl.BlockSpec((1, DV_A), const),
            pl.BlockSpec((1, D_MODEL), const),
            pl.BlockSpec((W_A, D_MODEL), const, pipeline_mode=single),
            pl.BlockSpec((W_B, D_MODEL), const, pipeline_mode=single),
            pl.BlockSpec((D_MODEL, D_MODEL), const, pipeline_mode=single),
        ],
        out_specs=pl.BlockSpec((tm, D_MODEL), lambda i: (i, 0)),
        out_shape=jax.ShapeDtypeStruct((t, D_MODEL), F32),
        compiler_params=_cparams(("parallel",)),
        name="merge_out",
    )(o_f, o_b, proj, attn, proj, proj, proj, x2d, onorm, nf, wpa, wpb, wout)


def _prep_weights(norm_in, w_in, conv_w, a_log_f, dt_bias_f, a_log_b, dt_bias_b, o_norm_a,
                  q_a_norm, w_q_b, kv_a_norm, w_kv_b, w_pa, w_pb, w_out, norm_f):
    w = w_in
    o = np.cumsum((0, CONV_DIM, W_A, H_A, H_A, H_A, H_A, Q_LORA, KV_LORA, D_ROPE, W_B, D_MODEL, D_MODEL))
    seg = lambda i: w[:, int(o[i]):int(o[i + 1])]
    qkv, gate_a, a_f, a_b, b_f, b_b, q_lat, kv_lat, k_rope, gate_b, gm_a, gm_b = (seg(i) for i in range(12))
    half = D_ROPE // 2
    k_rope_sw = jnp.concatenate([k_rope[:, half:], k_rope[:, :half]], axis=1)
    pad = jnp.zeros((D_MODEL, LANE - 4 * H_A), w.dtype)
    w_proj = jnp.concatenate([qkv, gate_a, gm_a, gm_b, gate_b, q_lat, kv_lat, k_rope, k_rope_sw,
                              a_f, a_b, b_f, b_b, pad], axis=1).astype(BF16)
    lane_pad = lambda f, bk: jnp.concatenate([f, bk, jnp.zeros((LANE - 2 * H_A,), F32)])[None, :]
    wq = w_q_b.reshape(Q_LORA, H_B, D_NOPE + D_ROPE)
    wq_rope = wq[:, :, D_NOPE:]
    wq_rope_sw = jnp.concatenate([wq_rope[..., half:], wq_rope[..., :half]], axis=-1)
    wkv = w_kv_b.reshape(KV_LORA, H_B, D_NOPE + DV_B)
    return dict(
        norm_in=norm_in[None, :], w_proj=w_proj, conv_w=conv_w,
        alog=lane_pad(a_log_f, a_log_b), dt=lane_pad(dt_bias_f, dt_bias_b),
        onorm=o_norm_a[None, :], nf=norm_f[None, :],
        gq=q_a_norm[None, :], gkv=kv_a_norm[None, :],
        wqn=wq[:, :, :D_NOPE].reshape(Q_LORA, W_B).astype(BF16),
        wqr=jnp.concatenate([wq_rope, wq_rope_sw], axis=-1).reshape(Q_LORA, W_B).astype(BF16),
        wkn=wkv[:, :, :D_NOPE].reshape(KV_LORA, W_B).astype(BF16),
        wv=wkv[:, :, D_NOPE:].reshape(KV_LORA, W_B).astype(BF16),
        wpa=w_pa.astype(BF16), wpb=w_pb.astype(BF16), wout=w_out.astype(BF16),
    )


def _rope_tables(s):
    pos = jnp.arange(s, dtype=F32)
    inv_freq = ROPE_THETA ** (-jnp.arange(0, D_ROPE, 2, dtype=F32) / D_ROPE)
    ang = pos[:, None] * inv_freq[None, :]
    cos, sin = jnp.cos(ang), jnp.sin(ang)
    z = jnp.zeros((s, LANE // 2), F32)
    return jnp.concatenate([cos, cos, z], axis=1), jnp.concatenate([-sin, sin, z], axis=1)


def _trunk(x, p):
    b, s, _ = x.shape
    x2d = x.reshape(b * s, D_MODEL)
    proj, small = _in_proj(x2d, p["norm_in"], p["w_proj"])
    qkv_act = _conv_act(proj, p["conv_w"], b, s)
    cols, rows = _gate_prep(small, p["alog"], p["dt"])
    o_f, o_b = _delta(qkv_act, cols, rows, b, s)
    cosm, sinm = _rope_tables(s)
    qp, kp, vp = _mla_prep(proj, small, cosm, sinm, p["gq"], p["gkv"], p["wqn"], p["wqr"], p["wkn"], p["wv"], b, s)
    attn = _attention(qp, kp, vp, b, s)
    y = _merge_out(o_f, o_b, proj, attn, x2d, p["onorm"], p["nf"], p["wpa"], p["wpb"], p["wout"])
    return y.reshape(b, s, D_MODEL)


def kernel(x_prompt, x_sample, norm_in, w_in, conv_w, a_log_f, dt_bias_f, a_log_b, dt_bias_b, o_norm_a,
           q_a_norm, w_q_b, kv_a_norm, w_kv_b, w_pa, w_pb, w_out, norm_f):
    assert norm_in.shape[0] == 1, "single-layer trunk"
    p = _prep_weights(norm_in[0], w_in[0], conv_w[0], a_log_f[0], dt_bias_f[0], a_log_b[0], dt_bias_b[0],
                      o_norm_a[0], q_a_norm[0], w_q_b[0], kv_a_norm[0], w_kv_b[0], w_pa[0], w_pb[0], w_out[0],
                      norm_f)
    return (_trunk(x_prompt, p), _trunk(x_sample, p))
```
